```python
import jax, jax.numpy as jnp
from jax import lax
import numpy as np

D_MODEL = 2048
BATCH = 4
SEQ = 8192
DEPTH = 1

MLA_HEADS = 8
Q_LORA_RANK = 512
KV_LORA_RANK = 512
QK_NOPE_DIM = 128
QK_ROPE_DIM = 64
V_HEAD_DIM = 128
ROPE_THETA = 10000.0
FOX_HEADS = 8
FOX_HEAD_DIM = 128
MLA_WIDTH = MLA_HEADS * V_HEAD_DIM
FOX_WIDTH = FOX_HEADS * FOX_HEAD_DIM
MIX_WIDTH = MLA_WIDTH + FOX_WIDTH
Q_BLOCK = 128
IN_SPLITS = (Q_LORA_RANK, KV_LORA_RANK, QK_ROPE_DIM, FOX_WIDTH, FOX_WIDTH, FOX_WIDTH, FOX_HEADS)
IN_WIDTH = sum(IN_SPLITS)
N_EXPERTS = 32
TOP_K = 4
D_EXPERT = D_MODEL
SWIGLU_LIMIT = 7.0
SWIGLU_ALPHA = 1.702
EXPERT_BLOCK = 256
NORM_EPS = 1e-5

kernel_name = "hybrid_mla_fox_moe_layer"


def rms_norm(x, g):
    xf = x.astype(jnp.float32)
    y = xf * lax.rsqrt(jnp.mean(xf * xf, axis=-1, keepdims=True) + NORM_EPS)
    return (y * g.astype(jnp.float32)).astype(x.dtype)


def apply_rope(x, pos):
    half = x.shape[-1] // 2
    inv_freq = ROPE_THETA ** (-jnp.arange(half, dtype=jnp.float32) / half)
    ang = pos.astype(jnp.float32)[:, None] * inv_freq[None, :]
    cos = jnp.cos(ang)[None, :, None, :]
    sin = jnp.sin(ang)[None, :, None, :]
    xf = x.astype(jnp.float32)
    x1, x2 = xf[..., :half], xf[..., half:]
    return jnp.concatenate([x1 * cos - x2 * sin, x2 * cos + x1 * sin], axis=-1).astype(x.dtype)


def causal_block_attention(q, k, v, scale, cum_logf=None):
    B, H, S, _ = q.shape
    Dv = v.shape[-1]
    n_blocks = S // Q_BLOCK
    k_pos = jnp.arange(S)

    def one_block(i):
        start = i * Q_BLOCK
        q_blk = lax.dynamic_slice_in_dim(q, start, Q_BLOCK, axis=2)
        s = jnp.einsum('bhqd,bhkd->bhqk', q_blk, k).astype(jnp.float32) * scale
        if cum_logf is not None:
            c_q = lax.dynamic_slice_in_dim(cum_logf, start, Q_BLOCK, axis=2)
            s = s + (c_q[..., :, None] - cum_logf[..., None, :])
        q_pos = start + jnp.arange(Q_BLOCK)
        s = jnp.where(k_pos[None, :] <= q_pos[:, None], s, -jnp.inf)
        p = jax.nn.softmax(s, axis=-1)
        return jnp.einsum('bhqk,bhkd->bhqd', p.astype(v.dtype), v)

    o = lax.map(one_block, jnp.arange(n_blocks))
    return o.transpose(1, 2, 0, 3, 4).reshape(B, H, S, Dv)


def hybrid_mixer(x, pos, g_attn, w_in, g_q_a, w_q_b, g_kv_a, w_kv_b, b_forget,
                 g_mla_out, g_fox_out, w_o):
    B, S, _ = x.shape
    hn = rms_norm(x, g_attn)
    proj = hn @ w_in
    offsets = np.cumsum(np.array(IN_SPLITS))[:-1].tolist()
    q_c, kv_c, k_rope, fq, fk, fv, f_logit = jnp.split(proj, offsets, axis=-1)

    q = (rms_norm(q_c, g_q_a) @ w_q_b).reshape(B, S, MLA_HEADS, QK_NOPE_DIM + QK_ROPE_DIM)
    q_nope, q_rope = q[..., :QK_NOPE_DIM], q[..., QK_NOPE_DIM:]
    kv = (rms_norm(kv_c, g_kv_a) @ w_kv_b).reshape(B, S, MLA_HEADS, QK_NOPE_DIM + V_HEAD_DIM)
    k_nope, v_mla = kv[..., :QK_NOPE_DIM], kv[..., QK_NOPE_DIM:]
    q_rope = apply_rope(q_rope, pos)
    k_rope = apply_rope(k_rope[:, :, None, :], pos)
    q_mla = jnp.concatenate([q_nope, q_rope], axis=-1).transpose(0, 2, 1, 3)
    k_mla = jnp.concatenate(
        [k_nope, jnp.broadcast_to(k_rope, (B, S, MLA_HEADS, QK_ROPE_DIM))], axis=-1
    ).transpose(0, 2, 1, 3)
    o_mla = causal_block_attention(q_mla, k_mla, v_mla.transpose(0, 2, 1, 3),
                                   (QK_NOPE_DIM + QK_ROPE_DIM) ** -0.5)
    o_mla = o_mla.transpose(0, 2, 1, 3).reshape(B, S, MLA_WIDTH)

    def heads(t):
        return t.reshape(B, S, FOX_HEADS, FOX_HEAD_DIM).transpose(0, 2, 1, 3)
    log_f = jax.nn.log_sigmoid(f_logit.astype(jnp.float32) + b_forget.astype(jnp.float32))
    cum_logf = jnp.cumsum(log_f, axis=1).transpose(0, 2, 1)
    o_fox = causal_block_attention(heads(fq), heads(fk), heads(fv), FOX_HEAD_DIM ** -0.5, cum_logf)
    o_fox = o_fox.transpose(0, 2, 1, 3).reshape(B, S, FOX_WIDTH)

    mixed = jnp.concatenate([rms_norm(o_mla, g_mla_out), rms_norm(o_fox, g_fox_out)], axis=-1)
    return mixed @ w_o


def moe_ffn(x, g_ffn, w_router, b_router, w_gate_up, b_gate_up, w_down, b_down):
    B, S, D = x.shape
    T = B * S
    x2d = rms_norm(x, g_ffn).reshape(T, D)
    logits = (x2d @ w_router).astype(jnp.float32) + b_router.astype(jnp.float32)
    top_val, top_idx = lax.top_k(logits, TOP_K)
    gates = jax.nn.softmax(top_val, axis=-1)

    M = T * TOP_K
    e_flat = top_idx.reshape(M)
    tok_flat = jnp.arange(M, dtype=jnp.int32) // TOP_K
    g_flat = gates.reshape(M)
    order = jnp.argsort(e_flat)
    e_s, t_s, g_s = e_flat[order], tok_flat[order], g_flat[order]
    counts = jnp.bincount(e_flat, length=N_EXPERTS)
    padded = (counts + EXPERT_BLOCK - 1) // EXPERT_BLOCK * EXPERT_BLOCK
    start = jnp.cumsum(counts) - counts
    pend = jnp.cumsum(padded)
    pstart = pend - padded
    dest = pstart[e_s] + (jnp.arange(M) - start[e_s])
    n_blocks = -(-(M + N_EXPERTS * (EXPERT_BLOCK - 1)) // EXPERT_BLOCK)
    P = n_blocks * EXPERT_BLOCK
    slot_tok = jnp.full((P,), T, jnp.int32).at[dest].set(t_s)
    slot_gate = jnp.zeros((P,), jnp.float32).at[dest].set(g_s)
    block_expert = jnp.minimum(
        jnp.searchsorted(pend, jnp.arange(n_blocks) * EXPERT_BLOCK, side='right'), N_EXPERTS - 1)
    x_pad = jnp.concatenate([x2d, jnp.zeros((1, D), x2d.dtype)], axis=0)

    def expert_block(args):
        e, toks, g = args
        xb = x_pad[toks]
        gu = xb @ w_gate_up[e] + b_gate_up[e]
        gate, up = gu[:, :D_EXPERT], gu[:, D_EXPERT:]
        gate = jnp.minimum(gate, SWIGLU_LIMIT)
        up = jnp.clip(up, -SWIGLU_LIMIT, SWIGLU_LIMIT)
        hid = gate * jax.nn.sigmoid(SWIGLU_ALPHA * gate) * (up + 1.0)
        return (hid @ w_down[e] + b_down[e]) * g[:, None].astype(xb.dtype)

    y_blocks = lax.map(expert_block, (block_expert,
                                      slot_tok.reshape(n_blocks, EXPERT_BLOCK),
                                      slot_gate.reshape(n_blocks, EXPERT_BLOCK)))
    y = jax.ops.segment_sum(y_blocks.reshape(P, D), slot_tok, num_segments=T + 1)[:T]
    return y.reshape(B, S, D)


def setup_inputs(seed: int = 0) -> dict:
    key = jax.random.key(seed)
    ks = jax.random.split(key, 20)
    f32 = jnp.float32
    L = DEPTH

    def nrm(k, shape, fan_in):
        return jax.random.normal(k, shape, f32) * fan_in ** -0.5

    def gain(k, shape):
        return 1.0 + 0.01 * jax.random.normal(k, shape, f32)

    def small(k, shape):
        return 0.01 * jax.random.normal(k, shape, f32)

    return {
        "x": jax.random.normal(ks[0], (BATCH, SEQ, D_MODEL), f32),
        "g_attn": gain(ks[1], (L, D_MODEL)),
        "w_in": nrm(ks[2], (L, D_MODEL, IN_WIDTH), D_MODEL),
        "g_q_a": gain(ks[3], (L, Q_LORA_RANK)),
        "w_q_b": nrm(ks[4], (L, Q_LORA_RANK, MLA_HEADS * (QK_NOPE_DIM + QK_ROPE_DIM)), Q_LORA_RANK),
        "g_kv_a": gain(ks[5], (L, KV_LORA_RANK)),
        "w_kv_b": nrm(ks[6], (L, KV_LORA_RANK, MLA_HEADS * (QK_NOPE_DIM + V_HEAD_DIM)), KV_LORA_RANK),
        "b_forget": jax.random.uniform(ks[7], (L, FOX_HEADS), f32, 1.0, 4.0),
        "g_mla_out": gain(ks[8], (L, MLA_WIDTH)),
        "g_fox_out": gain(ks[9], (L, FOX_WIDTH)),
        "w_o": nrm(ks[10], (L, MIX_WIDTH, D_MODEL), MIX_WIDTH),
        "g_ffn": gain(ks[11], (L, D_MODEL)),
        "w_router": nrm(ks[12], (L, D_MODEL, N_EXPERTS), D_MODEL),
        "b_router": small(ks[13], (L, N_EXPERTS)),
        "w_gate_up": nrm(ks[14], (L, N_EXPERTS, D_MODEL, 2 * D_EXPERT), D_MODEL),
        "b_gate_up": small(ks[15], (L, N_EXPERTS, 2 * D_EXPERT)),
        "w_down": nrm(ks[16], (L, N_EXPERTS, D_EXPERT, D_MODEL), D_EXPERT),
        "b_down": small(ks[17], (L, N_EXPERTS, D_MODEL)),
        "g_final": gain(ks[18], (D_MODEL,)),
    }


def reference(x, g_attn, w_in, g_q_a, w_q_b, g_kv_a, w_kv_b, b_forget, g_mla_out, g_fox_out,
              w_o, g_ffn, w_router, b_router, w_gate_up, b_gate_up, w_down, b_down, g_final):
    pos = jnp.arange(x.shape[1], dtype=jnp.int32)
    h = x
    for l in range(DEPTH):
        h = h + hybrid_mixer(h, pos, g_attn[l], w_in[l], g_q_a[l], w_q_b[l], g_kv_a[l], w_kv_b[l],
                             b_forget[l], g_mla_out[l], g_fox_out[l], w_o[l])
        h = h + moe_ffn(h, g_ffn[l], w_router[l], b_router[l], w_gate_up[l], b_gate_up[l],
                        w_down[l], b_down[l])
    return rms_norm(h, g_final)
```

```python
import functools
import math

import jax
import jax.numpy as jnp
import numpy as np
from jax import lax
from jax.experimental import pallas as pl
from jax.experimental.pallas import tpu as pltpu

F32 = jnp.float32
BF16 = jnp.bfloat16

D_MODEL = 2048
MLA_HEADS = 8
Q_LORA = 512
KV_LORA = 512
NOPE = 128
ROPE = 64
V_DIM = 128
ROPE_THETA = 10000.0
FOX_HEADS = 8
FOX_DIM = 128
HEADS = 8
HEAD_W = 128
GROUP_W = HEADS * HEAD_W
N_EXPERTS = 32
TOP_K = 4
D_EXPERT = 2048
SWIGLU_LIMIT = 7.0
SWIGLU_ALPHA = 1.702
NORM_EPS = 1e-5
LOG2E = math.log2(math.e)

LANES = 128
TOKEN_TILE = 512
ATTN_TILE = 512
EXPERT_TILE = 512
EXPERT_SUB = 256
F_CHUNK = 512
ROW_TILE = 256
VMEM_CAP = 60 * 1024 * 1024
NEG_BIG = -1e30

_NT = (((1,), (1,)), ((), ()))


def _vmem(nbytes):
    return int(min(VMEM_CAP, nbytes * 5 // 4 + (4 << 20)))


def _rms(x, g):
    ms = jnp.mean(x * x, axis=-1, keepdims=True)
    return x * lax.rsqrt(ms + NORM_EPS) * g


def _dot(a, b):
    return jnp.dot(a, b, preferred_element_type=F32)


def _resident(shape):
    n = len(shape)
    return pl.BlockSpec(shape, lambda *_: (0,) * n, pipeline_mode=pl.Buffered(1))


W_FQ, W_FK, W_LAT, W_TAIL = 0, GROUP_W, 2 * GROUP_W, 2 * GROUP_W + Q_LORA + KV_LORA
W_MAIN = W_TAIL + 2 * LANES


def _inproj_kernel(x_ref, g_ref, w_ref, wvt_ref, fq_ref, fk_ref, fvt_ref, lat_ref, tail_ref, *, qscale):
    xn = _rms(x_ref[...], g_ref[...]).astype(BF16)
    fq_ref[...] = (_dot(xn, w_ref[:, W_FQ:W_FK]) * qscale).astype(BF16)
    fk_ref[...] = _dot(xn, w_ref[:, W_FK:W_LAT]).astype(BF16)
    lat_ref[...] = _dot(xn, w_ref[:, W_LAT:W_TAIL])
    tail_ref[...] = _dot(xn, w_ref[:, W_TAIL:W_MAIN])
    fvt_ref[...] = lax.dot_general(wvt_ref[...], xn, _NT, preferred_element_type=F32).astype(BF16)


def _in_proj(x2, g_attn, w_main, w_fvt):
    T = x2.shape[0]
    tm = TOKEN_TILE
    row = lambda w: pl.BlockSpec((tm, w), lambda i: (i, 0))
    est = (2 * tm * D_MODEL * 4 + D_MODEL * W_MAIN * 2 + GROUP_W * D_MODEL * 2
           + 2 * (3 * tm * GROUP_W * 2 + tm * GROUP_W * 4 + tm * 2 * LANES * 4) + 6 * tm * GROUP_W * 4)
    return pl.pallas_call(
        functools.partial(_inproj_kernel, qscale=FOX_DIM ** -0.5 * LOG2E),
        grid=(T // tm,),
        in_specs=[row(D_MODEL), _resident((1, D_MODEL)), _resident((D_MODEL, W_MAIN)),
                  _resident((GROUP_W, D_MODEL))],
        out_specs=[row(GROUP_W), row(GROUP_W), pl.BlockSpec((GROUP_W, tm), lambda i: (0, i)),
                   row(GROUP_W), row(2 * LANES)],
        out_shape=[jax.ShapeDtypeStruct((T, GROUP_W), BF16), jax.ShapeDtypeStruct((T, GROUP_W), BF16),
                   jax.ShapeDtypeStruct((GROUP_W, T), BF16), jax.ShapeDtypeStruct((T, GROUP_W), F32),
                   jax.ShapeDtypeStruct((T, 2 * LANES), F32)],
        compiler_params=pltpu.CompilerParams(dimension_semantics=("arbitrary",), vmem_limit_bytes=_vmem(est)),
        name="in_proj",
    )(x2, g_attn, w_main, w_fvt)


def _rope(grp, cc, ss):
    return grp * cc + pltpu.roll(grp, ROPE // 2, 1) * ss


def _latproj_kernel(lat_ref, tail_ref, gq_ref, gkv_ref, wq_ref, wkn_ref, wvt_ref, cc_ref, ss_ref,
                    qn_ref, qr_ref, kn_ref, kr_ref, vt_ref, *, qscale):
    nq = _rms(lat_ref[:, :Q_LORA], gq_ref[...]).astype(BF16)
    nkv = _rms(lat_ref[:, Q_LORA:], gkv_ref[...]).astype(BF16)
    cc = cc_ref[...]
    ss = ss_ref[...]
    qn_ref[...] = (_dot(nq, wq_ref[:, :GROUP_W]) * qscale).astype(BF16)
    qrope = _dot(nq, wq_ref[:, GROUP_W:])
    for h in range(HEADS):
        cols = slice(h * HEAD_W, (h + 1) * HEAD_W)
        qr_ref[:, cols] = (_rope(qrope[:, cols], cc, ss) * qscale).astype(BF16)
    kn_ref[...] = _dot(nkv, wkn_ref[...]).astype(BF16)
    vt_ref[...] = lax.dot_general(wvt_ref[...], nkv, _NT, preferred_element_type=F32).astype(BF16)
    krot = _rope(tail_ref[:, :LANES], cc, ss)
    lane = lax.broadcasted_iota(jnp.int32, krot.shape, 1)
    kr_ref[...] = jnp.where(lane < ROPE, krot, 0.0).astype(BF16)


def _lat_proj(lat, tail, g_q, g_kv, wq, wkn, wvt, cc, ss, seq):
    T = lat.shape[0]
    tm = TOKEN_TILE
    per_seq = seq // tm
    row = lambda w: pl.BlockSpec((tm, w), lambda i: (i, 0))
    pos = pl.BlockSpec((tm, LANES), lambda i: (i % per_seq, 0))
    est = (2 * tm * (GROUP_W + 2 * LANES) * 4 + Q_LORA * 2 * GROUP_W * 2 + 2 * KV_LORA * GROUP_W * 2
           + 4 * tm * LANES * 4 + 2 * (4 * tm * GROUP_W * 2 + tm * LANES * 2) + 6 * tm * GROUP_W * 4)
    return pl.pallas_call(
        functools.partial(_latproj_kernel, qscale=(NOPE + ROPE) ** -0.5 * LOG2E),
        grid=(T // tm,),
        in_specs=[row(GROUP_W), row(2 * LANES), _resident((1, Q_LORA)), _resident((1, KV_LORA)),
                  _resident((Q_LORA, 2 * GROUP_W)), _resident((KV_LORA, GROUP_W)),
                  _resident((GROUP_W, KV_LORA)), pos, pos],
        out_specs=[row(GROUP_W), row(GROUP_W), row(GROUP_W), row(LANES),
                   pl.BlockSpec((GROUP_W, tm), lambda i: (0, i))],
        out_shape=[jax.ShapeDtypeStruct((T, GROUP_W), BF16), jax.ShapeDtypeStruct((T, GROUP_W), BF16),
                   jax.ShapeDtypeStruct((T, GROUP_W), BF16), jax.ShapeDtypeStruct((T, LANES), BF16),
                   jax.ShapeDtypeStruct((GROUP_W, T), BF16)],
        compiler_params=pltpu.CompilerParams(dimension_semantics=("arbitrary",), vmem_limit_bytes=_vmem(est)),
        name="lat_proj",
    )(lat, tail, g_q, g_kv, wq, wkn, wvt, cc, ss)


def _split3(x):
    hi = x.astype(BF16)
    r1 = x - hi.astype(F32)
    mid = r1.astype(BF16)
    lo = (r1 - mid.astype(F32)).astype(BF16)
    return hi, mid, lo


def _foxgate_kernel(tail_ref, bf_ref, tri_ref, sel_ref, aug_ref, carry_ref, *, per_seq):
    tm = tail_ref.shape[0]

    @pl.when(pl.program_id(0) % per_seq == 0)
    def _():
        carry_ref[...] = jnp.zeros_like(carry_ref)

    z = tail_ref[:, LANES:] + bf_ref[...]
    nlf = jnp.log1p(jnp.exp(-jnp.abs(z))) - jnp.minimum(z, 0.0)
    tri = tri_ref[...]
    hi, mid, lo = _split3(nlf)
    decay = _dot(tri, hi) + _dot(tri, mid) + _dot(tri, lo) + carry_ref[...]
    carry_ref[...] = decay[tm - 1:tm, :]
    pieces = jnp.concatenate(_split3(decay * LOG2E), axis=1)
    aug_ref[...] = _dot(pieces, sel_ref[...]).astype(BF16)


def _fox_gate(tail, bf_row, tri, sel, seq):
    T = tail.shape[0]
    tm = TOKEN_TILE
    est = 2 * tm * 2 * LANES * 4 + tm * tm * 2 + 3 * LANES * GROUP_W * 2 + 2 * tm * GROUP_W * 2 + 8 * tm * LANES * 4
    return pl.pallas_call(
        functools.partial(_foxgate_kernel, per_seq=seq // tm),
        grid=(T // tm,),
        in_specs=[pl.BlockSpec((tm, 2 * LANES), lambda i: (i, 0)), _resident((1, LANES)),
                  _resident((tm, tm)), _resident((3 * LANES, GROUP_W))],
        out_specs=pl.BlockSpec((tm, GROUP_W), lambda i: (i, 0)),
        out_shape=jax.ShapeDtypeStruct((T, GROUP_W), BF16),
        scratch_shapes=[pltpu.VMEM((1, LANES), F32)],
        compiler_params=pltpu.CompilerParams(dimension_semantics=("arbitrary",), vmem_limit_bytes=_vmem(est)),
        name="fox_gate",
    )(tail, bf_row, tri, sel)


def _attn_kernel(qi_ref, kj_ref, qa_ref, qb_ref, ka_ref, kb_ref, vt_ref, o_ref, acc_ref, m_ref, l_ref,
                 *, qb_per_head, kb_per_head):
    step = pl.program_id(1)
    qi = qi_ref[step]
    kj = kj_ref[step]
    tq = qa_ref.shape[0]
    tk = ka_ref.shape[0]

    @pl.when(kj == 0)
    def _():
        m_ref[...] = jnp.full_like(m_ref, NEG_BIG)
        l_ref[...] = jnp.zeros_like(l_ref)
        acc_ref[...] = jnp.zeros_like(acc_ref)

    def heads(diagonal):
        for h in range(HEADS):
            cols = slice(h * HEAD_W, (h + 1) * HEAD_W)
            q = jnp.concatenate([qa_ref[:, cols], qb_ref[:, cols] if qb_per_head else qb_ref[...]], axis=1)
            k = jnp.concatenate([ka_ref[:, cols], kb_ref[:, cols] if kb_per_head else kb_ref[...]], axis=1)
            st = lax.dot_general(k, q, _NT, preferred_element_type=F32)
            if diagonal:
                kpos = lax.broadcasted_iota(jnp.int32, (tk, tq), 0)
                qpos = lax.broadcasted_iota(jnp.int32, (tk, tq), 1)
                st = jnp.where(kpos > qpos, NEG_BIG, st)
            m_prev = m_ref[h]
            m_new = jnp.maximum(m_prev, jnp.max(st, axis=0, keepdims=True))
            alpha = jnp.exp2(m_prev - m_new)
            p = jnp.exp2(st - m_new)
            l_ref[h] = alpha * l_ref[h] + jnp.sum(p, axis=0, keepdims=True)
            m_ref[h] = m_new
            pv = _dot(vt_ref[cols, :], p.astype(BF16))
            acc_ref[h] = acc_ref[h] * alpha + pv

    @pl.when(kj < qi)
    def _():
        heads(False)

    @pl.when(kj == qi)
    def _():
        heads(True)
        for h in range(HEADS):
            cols = slice(h * HEAD_W, (h + 1) * HEAD_W)
            o_ref[:, cols] = (acc_ref[h] * (1.0 / l_ref[h])).T


def _attention(qa, qb, ka, kb, vt, batch, seq, *, qb_per_head, kb_per_head, name):
    T = qa.shape[0]
    t = ATTN_TILE
    n = seq // t
    qi_np, kj_np = zip(*[(i, j) for i in range(n) for j in range(i + 1)])
    qi_arr = jnp.asarray(np.array(qi_np, np.int32))
    kj_arr = jnp.asarray(np.array(kj_np, np.int32))
    qrow = lambda w: pl.BlockSpec((t, w), lambda b, s, qi, kj: (b * n + qi[s], 0))
    krow = lambda w: pl.BlockSpec((t, w), lambda b, s, qi, kj: (b * n + kj[s], 0))
    qb_spec = qrow(GROUP_W) if qb_per_head else pl.BlockSpec((t, HEAD_W), lambda b, s, qi, kj: (0, 0))
    kb_spec = krow(GROUP_W) if kb_per_head else krow(HEAD_W)
    est = (2 * 2 * t * GROUP_W * 2 * 3 + 2 * t * GROUP_W * 4 + HEADS * HEAD_W * t * 4
           + 12 * t * t * 4)
    return pl.pallas_call(
        functools.partial(_attn_kernel, qb_per_head=qb_per_head, kb_per_head=kb_per_head),
        grid_spec=pltpu.PrefetchScalarGridSpec(
            num_scalar_prefetch=2,
            grid=(batch, len(qi_np)),
            in_specs=[qrow(GROUP_W), qb_spec, krow(GROUP_W), kb_spec,
                      pl.BlockSpec((GROUP_W, t), lambda b, s, qi, kj: (0, b * n + kj[s]))],
            out_specs=qrow(GROUP_W),
            scratch_shapes=[pltpu.VMEM((HEADS, HEAD_W, t), F32), pltpu.VMEM((HEADS, 1, t), F32),
                            pltpu.VMEM((HEADS, 1, t), F32)],
        ),
        out_shape=jax.ShapeDtypeStruct((T, GROUP_W), F32),
        compiler_params=pltpu.CompilerParams(dimension_semantics=("arbitrary", "arbitrary"),
                                             vmem_limit_bytes=_vmem(est)),
        name=name,
    )(qi_arr, kj_arr, qa, qb, ka, kb, vt)


def _outproj_kernel(om_ref, of_ref, gm_ref, gf_ref, wo_ref, x_ref, gffn_ref, wr_ref, br_ref,
                    h_ref, hn_ref, idx_ref, gate_ref):
    mixed = jnp.concatenate([_rms(om_ref[...], gm_ref[...]), _rms(of_ref[...], gf_ref[...])], axis=1)
    h = x_ref[...] + _dot(mixed.astype(BF16), wo_ref[...])
    h_ref[...] = h
    hn = _rms(h, gffn_ref[...])
    hn_ref[...] = hn
    logits = jnp.dot(hn, wr_ref[...], preferred_element_type=F32, precision=lax.Precision.HIGHEST) + br_ref[...]
    lane = lax.broadcasted_iota(jnp.int32, logits.shape, 1)
    cur = jnp.where(lane < N_EXPERTS, logits, -jnp.inf)
    idx_out = jnp.zeros(logits.shape, jnp.int32)
    val_out = jnp.zeros(logits.shape, F32)
    top = None
    denom = None
    for r in range(TOP_K):
        m = jnp.max(cur, axis=1, keepdims=True)
        idx = jnp.min(jnp.where(cur == m, lane, LANES), axis=1, keepdims=True)
        if r == 0:
            top = m
        e = jnp.exp(m - top)
        denom = e if r == 0 else denom + e
        idx_out = jnp.where(lane == r, idx, idx_out)
        val_out = jnp.where(lane == r, e, val_out)
        cur = jnp.where(lane == idx, -jnp.inf, cur)
    idx_ref[...] = idx_out
    gate_ref[...] = val_out / denom


def _out_proj(o_mla, o_fox, g_m, g_f, w_o, x2, g_ffn, w_r, b_r):
    T = x2.shape[0]
    tm = TOKEN_TILE
    row = lambda w: pl.BlockSpec((tm, w), lambda i: (i, 0))
    est = (2 * (2 * tm * GROUP_W * 4 + 3 * tm * D_MODEL * 4 + 2 * tm * LANES * 4) + D_MODEL * D_MODEL * 2
           + D_MODEL * LANES * 4 + 6 * tm * D_MODEL * 4)
    return pl.pallas_call(
        _outproj_kernel,
        grid=(T // tm,),
        in_specs=[row(GROUP_W), row(GROUP_W), _resident((1, GROUP_W)), _resident((1, GROUP_W)),
                  _resident((D_MODEL, D_MODEL)), row(D_MODEL), _resident((1, D_MODEL)),
                  _resident((D_MODEL, LANES)), _resident((1, LANES))],
        out_specs=[row(D_MODEL), row(D_MODEL), row(LANES), row(LANES)],
        out_shape=[jax.ShapeDtypeStruct((T, D_MODEL), F32), jax.ShapeDtypeStruct((T, D_MODEL), F32),
                   jax.ShapeDtypeStruct((T, LANES), jnp.int32), jax.ShapeDtypeStruct((T, LANES), F32)],
        compiler_params=pltpu.CompilerParams(dimension_semantics=("arbitrary",), vmem_limit_bytes=_vmem(est)),
        name="out_proj",
    )(o_mla, o_fox, g_m, g_f, w_o, x2, g_ffn, w_r, b_r)


def _row_copy(src_ref, src_row, dst_ref, dst_row, sem):
    return pltpu.make_async_copy(src_ref.at[pl.ds(src_row, 1), :], dst_ref.at[pl.ds(dst_row, 1), :], sem)


def _dispatch_kernel(dest_ref, hn_ref, xs_in_ref, xs_ref, sem):
    del xs_in_ref
    tm = hn_ref.shape[0]

    def start(t, c):
        for k in range(TOP_K):
            _row_copy(hn_ref, t, xs_ref, dest_ref[0, 0, t * TOP_K + k], sem).start()
        return c

    def wait(t, c):
        for k in range(TOP_K):
            _row_copy(hn_ref, 0, xs_ref, 0, sem).wait()
        return c

    lax.fori_loop(0, tm, start, 0)
    lax.fori_loop(0, tm, wait, 0)


def _dispatch(dest, hn, n_slots):
    T = hn.shape[0]
    tm = ROW_TILE
    dest3 = dest.reshape(T // tm, 1, tm * TOP_K)
    xs0 = jnp.zeros((n_slots, D_MODEL), F32)
    return pl.pallas_call(
        _dispatch_kernel,
        grid=(T // tm,),
        in_specs=[pl.BlockSpec((1, 1, tm * TOP_K), lambda i: (i, 0, 0), memory_space=pltpu.SMEM),
                  pl.BlockSpec((tm, D_MODEL), lambda i: (i, 0)),
                  pl.BlockSpec(memory_space=pl.ANY)],
        out_specs=pl.BlockSpec(memory_space=pl.ANY),
        out_shape=jax.ShapeDtypeStruct((n_slots, D_MODEL), F32),
        scratch_shapes=[pltpu.SemaphoreType.DMA(())],
        input_output_aliases={2: 0},
        compiler_params=pltpu.CompilerParams(dimension_semantics=("arbitrary",), has_side_effects=True),
        name="dispatch",
    )(dest3, hn, xs0)


def _combine_kernel(dest_ref, h_ref, gate_ref, gfin_ref, ys_ref, o_ref, buf_ref, sem):
    tm = h_ref.shape[0]

    def start(t, c):
        for k in range(TOP_K):
            _row_copy(ys_ref, dest_ref[0, 0, t * TOP_K + k], buf_ref.at[k], t, sem).start()
        return c

    def wait(t, c):
        for k in range(TOP_K):
            _row_copy(ys_ref, 0, buf_ref.at[k], 0, sem).wait()
        return c

    lax.fori_loop(0, tm, start, 0)
    lax.fori_loop(0, tm, wait, 0)
    y = h_ref[...]
    gates = gate_ref[...]
    for k in range(TOP_K):
        y = y + buf_ref[k] * gates[:, k:k + 1]
    o_ref[...] = _rms(y, gfin_ref[...])


def _combine(dest, h, gates, g_final, ys):
    T = h.shape[0]
    tm = ROW_TILE
    dest3 = dest.reshape(T // tm, 1, tm * TOP_K)
    est = 2 * (2 * tm * D_MODEL * 4 + tm * LANES * 4) + TOP_K * tm * D_MODEL * 4 + 4 * tm * D_MODEL * 4
    return pl.pallas_call(
        _combine_kernel,
        grid=(T // tm,),
        in_specs=[pl.BlockSpec((1, 1, tm * TOP_K), lambda i: (i, 0, 0), memory_space=pltpu.SMEM),
                  pl.BlockSpec((tm, D_MODEL), lambda i: (i, 0)),
                  pl.BlockSpec((tm, LANES), lambda i: (i, 0)),
                  _resident((1, D_MODEL)),
                  pl.BlockSpec(memory_space=pl.ANY)],
        out_specs=pl.BlockSpec((tm, D_MODEL), lambda i: (i, 0)),
        out_shape=jax.ShapeDtypeStruct((T, D_MODEL), F32),
        scratch_shapes=[pltpu.VMEM((TOP_K, tm, D_MODEL), F32), pltpu.SemaphoreType.DMA(())],
        compiler_params=pltpu.CompilerParams(dimension_semantics=("arbitrary",), vmem_limit_bytes=_vmem(est)),
        name="combine",
    )(dest3, h, gates, g_final, ys)


def _expert_kernel(te_ref, tv_ref, xs_ref, wg_ref, wu_ref, bg_ref, bu_ref, wd_ref, bd_ref, ys_ref, xb_ref):
    j = pl.program_id(0)
    fc = pl.program_id(1)
    valid = tv_ref[j]

    @pl.when(fc == 0)
    def _():
        xb_ref[...] = xs_ref[...].astype(BF16)

    for sub in range(EXPERT_TILE // EXPERT_SUB):
        rows = slice(sub * EXPERT_SUB, (sub + 1) * EXPERT_SUB)

        @pl.when(sub * EXPERT_SUB < valid)
        def _():
            xb = xb_ref[rows, :]
            gate = jnp.minimum(_dot(xb, wg_ref[...]) + bg_ref[...], SWIGLU_LIMIT)
            up = jnp.clip(_dot(xb, wu_ref[...]) + bu_ref[...], -SWIGLU_LIMIT, SWIGLU_LIMIT)
            hid = gate * jax.nn.sigmoid(SWIGLU_ALPHA * gate) * (up + 1.0)
            part = _dot(hid.astype(BF16), wd_ref[...])

            @pl.when(fc == 0)
            def _():
                ys_ref[rows, :] = part + bd_ref[...]

            @pl.when(fc > 0)
            def _():
                ys_ref[rows, :] += part

        @pl.when(jnp.logical_and(sub * EXPERT_SUB >= valid, fc == 0))
        def _():
            ys_ref[rows, :] = jnp.zeros((EXPERT_SUB, D_MODEL), F32)


def _experts(tile_expert, tile_valid, xs, w_gu, b_gu, w_d, b_d):
    n_tiles = tile_expert.shape[0]
    tm = EXPERT_TILE
    fcn = D_EXPERT // F_CHUNK
    est = (2 * 2 * tm * D_MODEL * 4 + 2 * 3 * D_MODEL * F_CHUNK * 2 + tm * D_MODEL * 2
           + 6 * EXPERT_SUB * D_MODEL * 4)
    return pl.pallas_call(
        _expert_kernel,
        grid_spec=pltpu.PrefetchScalarGridSpec(
            num_scalar_prefetch=2,
            grid=(n_tiles, fcn),
            in_specs=[pl.BlockSpec((tm, D_MODEL), lambda j, f, te, tv: (j, 0)),
                      pl.BlockSpec((None, D_MODEL, F_CHUNK), lambda j, f, te, tv: (te[j], 0, f)),
                      pl.BlockSpec((None, D_MODEL, F_CHUNK), lambda j, f, te, tv: (te[j], 0, fcn + f)),
                      pl.BlockSpec((None, 1, F_CHUNK), lambda j, f, te, tv: (te[j], 0, f)),
                      pl.BlockSpec((None, 1, F_CHUNK), lambda j, f, te, tv: (te[j], 0, fcn + f)),
                      pl.BlockSpec((None, F_CHUNK, D_MODEL), lambda j, f, te, tv: (te[j], f, 0)),
                      pl.BlockSpec((None, 1, D_MODEL), lambda j, f, te, tv: (te[j], 0, 0))],
            out_specs=pl.BlockSpec((tm, D_MODEL), lambda j, f, te, tv: (j, 0)),
            scratch_shapes=[pltpu.VMEM((tm, D_MODEL), BF16)],
        ),
        out_shape=jax.ShapeDtypeStruct((n_tiles * tm, D_MODEL), F32),
        compiler_params=pltpu.CompilerParams(dimension_semantics=("arbitrary", "arbitrary"),
                                             vmem_limit_bytes=_vmem(est)),
        name="experts",
    )(tile_expert, tile_valid, xs, w_gu, w_gu, b_gu, b_gu, w_d, b_d)


def _routing_plan(top_idx, n_tiles):
    experts = jnp.arange(N_EXPERTS, dtype=jnp.int32)
    onehot = (top_idx[:, :, None] == experts).astype(jnp.int32).sum(axis=1)
    incl = jnp.cumsum(onehot, axis=0)
    counts = incl[-1]
    rank = jnp.take_along_axis(incl - onehot, top_idx, axis=1)
    tiles_e = (counts + EXPERT_TILE - 1) // EXPERT_TILE
    tile_end = jnp.cumsum(tiles_e)
    tile_start = tile_end - tiles_e
    dest = (tile_start * EXPERT_TILE)[top_idx] + rank
    tile = jnp.arange(n_tiles, dtype=jnp.int32)
    last = jnp.max(jnp.where(counts > 0, experts, 0))
    tile_expert = jnp.minimum(jnp.searchsorted(tile_end, tile, side="right").astype(jnp.int32), last)
    used = tile < tile_end[-1]
    tile_valid = jnp.where(
        used, jnp.clip(counts[tile_expert] - (tile - tile_start[tile_expert]) * EXPERT_TILE, 0, EXPERT_TILE), 0)
    return dest.astype(jnp.int32), tile_expert, tile_valid.astype(jnp.int32)


def _rope_tables(seq):
    half = ROPE // 2
    inv_freq = ROPE_THETA ** (-jnp.arange(half, dtype=F32) / half)
    ang = jnp.arange(seq, dtype=jnp.int32).astype(F32)[:, None] * inv_freq[None, :]
    cos, sin = jnp.cos(ang), jnp.sin(ang)
    return jnp.concatenate([cos] * 4, axis=1), jnp.concatenate([-sin, sin] * 2, axis=1)


def kernel(x, g_attn, w_in, g_q_a, w_q_b, g_kv_a, w_kv_b, b_forget, g_mla_out, g_fox_out, w_o, g_ffn,
           w_router, b_router, w_gate_up, b_gate_up, w_down, b_down, g_final):
    B, S, D = x.shape
    assert D == D_MODEL and S % TOKEN_TILE == 0 and S % ATTN_TILE == 0 and g_attn.shape[0] == 1
    T = B * S
    x2 = x.reshape(T, D)
    row = lambda v: v.reshape(1, -1).astype(F32)

    w = w_in[0]
    o_q, o_kv, o_kr = 0, Q_LORA, Q_LORA + KV_LORA
    o_fq = o_kr + ROPE
    o_fk, o_fv, o_fl = o_fq + GROUP_W, o_fq + 2 * GROUP_W, o_fq + 3 * GROUP_W
    w_kr = w[:, o_kr:o_fq]
    tail = jnp.concatenate([w_kr, w_kr, w[:, o_fl:o_fl + FOX_HEADS],
                            jnp.zeros((D, LANES - FOX_HEADS), F32)], axis=1)
    w_main = jnp.concatenate([w[:, o_fq:o_fk], w[:, o_fk:o_fv], w[:, o_q:o_kr], tail], axis=1).astype(BF16)
    w_fvt = w[:, o_fv:o_fl].T.astype(BF16)

    wq = w_q_b[0].reshape(Q_LORA, MLA_HEADS, NOPE + ROPE)
    wq_rope = wq[:, :, NOPE:]
    wq_all = jnp.concatenate([wq[:, :, :NOPE].reshape(Q_LORA, GROUP_W),
                              jnp.concatenate([wq_rope, wq_rope], axis=2).reshape(Q_LORA, GROUP_W)],
                             axis=1).astype(BF16)
    wkv = w_kv_b[0].reshape(KV_LORA, MLA_HEADS, NOPE + V_DIM)
    wkn = wkv[:, :, :NOPE].reshape(KV_LORA, GROUP_W).astype(BF16)
    wvt = wkv[:, :, NOPE:].reshape(KV_LORA, GROUP_W).T.astype(BF16)

    cc, ss = _rope_tables(S)
    bf_row = jnp.concatenate([b_forget[0].astype(F32), jnp.zeros((LANES - FOX_HEADS,), F32)]).reshape(1, LANES)
    tri = jnp.tril(jnp.ones((TOKEN_TILE, TOKEN_TILE), F32)).astype(BF16)
    sel_np = np.zeros((3 * LANES, GROUP_W), np.float32)
    for piece in range(3):
        for h in range(FOX_HEADS):
            sel_np[piece * LANES + h, h * HEAD_W + piece] = 1.0
    sel = jnp.asarray(sel_np).astype(BF16)
    ones_np = np.zeros((ATTN_TILE, HEAD_W), np.float32)
    ones_np[:, :3] = 1.0
    q_ones = jnp.asarray(ones_np).astype(BF16)

    w_r = jnp.concatenate([w_router[0], jnp.zeros((D, LANES - N_EXPERTS), F32)], axis=1)
    b_r = jnp.concatenate([b_router[0].astype(F32), jnp.zeros((LANES - N_EXPERTS,), F32)]).reshape(1, LANES)

    fq, fk, fvt, lat, tail_act = _in_proj(x2, row(g_attn), w_main, w_fvt)
    qn, qr, kn, kr, vt = _lat_proj(lat, tail_act, row(g_q_a), row(g_kv_a), wq_all, wkn, wvt, cc, ss, S)
    aug = _fox_gate(tail_act, bf_row, tri, sel, S)
    o_mla = _attention(qn, qr, kn, kr, vt, B, S, qb_per_head=True, kb_per_head=False, name="attn_mla")
    o_fox = _attention(fq, q_ones, fk, aug, fvt, B, S, qb_per_head=False, kb_per_head=True, name="attn_fox")
    h, hn, idx128, gate128 = _out_proj(o_mla, o_fox, row(g_mla_out), row(g_fox_out), w_o[0].astype(BF16),
                                       x2, row(g_ffn), w_r, b_r)

    n_tiles = (T * TOP_K) // EXPERT_TILE + N_EXPERTS
    dest, tile_expert, tile_valid = _routing_plan(idx128[:, :TOP_K], n_tiles)
    xs = _dispatch(dest, hn, n_tiles * EXPERT_TILE)
    ys = _experts(tile_expert, tile_valid, xs, w_gate_up[0].astype(BF16),
                  b_gate_up[0].reshape(N_EXPERTS, 1, 2 * D_EXPERT).astype(F32),
                  w_down[0].astype(BF16), b_down[0].reshape(N_EXPERTS, 1, D_MODEL).astype(F32))
    out = _combine(dest, h, gate128, row(g_final), ys)
    return out.reshape(B, S, D)
```
